```python
import jax, jax.numpy as jnp
from jax import lax
import numpy as np

D_MODEL = 1024
BATCH = 8
SEQ = 4096
DEPTH = 1

PLE_DIM = 256
D_MIX = D_MODEL
D_CONV = D_MIX // 2
D_REC = D_MIX - D_CONV
CONV_GROUPS = 8
CONV_GROUP_DIM = D_CONV // CONV_GROUPS
CONV_WIDTH = 31
REC_HEAD_DIM = 128
N_REC_HEADS = D_REC // REC_HEAD_DIM
CHUNK = 64
D_FF = 4 * D_MODEL
D_IN_PROJ = 2 * D_CONV + 4 * D_REC
EPS = 1e-6

kernel_name = "hymba_conformer_hgrn2_hybrid"


def rmsnorm(x, g):
    xf = x.astype(jnp.float32)
    y = xf * lax.rsqrt(jnp.mean(xf * xf, axis=-1, keepdims=True) + EPS)
    return (y * g.astype(jnp.float32)).astype(x.dtype)


def group_layernorm(x, g, b):
    Bs, S, C = x.shape
    xf = x.astype(jnp.float32).reshape(Bs, S, CONV_GROUPS, CONV_GROUP_DIM)
    mu = jnp.mean(xf, axis=-1, keepdims=True)
    var = jnp.mean(jnp.square(xf - mu), axis=-1, keepdims=True)
    y = ((xf - mu) * lax.rsqrt(var + EPS)).reshape(Bs, S, C)
    return (y * g.astype(jnp.float32) + b.astype(jnp.float32)).astype(x.dtype)


def causal_depthwise_conv(x, w, b):
    C = x.shape[-1]
    y = lax.conv_general_dilated(
        x, w[:, None, :].astype(x.dtype), window_strides=(1,),
        padding=[(CONV_WIDTH - 1, 0)], dimension_numbers=("NWC", "WIO", "NWC"),
        feature_group_count=C)
    return y + b.astype(x.dtype)


def hgrn2_chunkwise(q, k, v, logf):
    Bs, S, H, dk = q.shape
    dv = v.shape[-1]
    n = S // CHUNK

    def to_chunks(t):
        return t.reshape(Bs, n, CHUNK, H, t.shape[-1]).transpose(1, 0, 3, 2, 4)

    qc, kc, vc, gc = map(to_chunks, (q, k, v, logf))
    bc = jnp.cumsum(gc, axis=3)
    causal = jnp.tril(jnp.ones((CHUNK, CHUNK), dtype=bool))[:, :, None]

    def step(state, inp):
        q_, k_, v_, b_ = inp
        diff = b_[:, :, :, None, :] - b_[:, :, None, :, :]
        decay = jnp.exp(jnp.where(causal, diff, -jnp.inf))
        attn = jnp.einsum('bhtd,bhsd,bhtsd->bhts', q_, k_, decay)
        o = (jnp.einsum('bhts,bhsv->bhtv', attn, v_)
             + jnp.einsum('bhtd,bhdv->bhtv', q_ * jnp.exp(b_), state))
        b_last = b_[:, :, -1:, :]
        new_state = (jnp.exp(b_last[:, :, 0, :])[..., None] * state
                     + jnp.einsum('bhsd,bhsv->bhdv', k_ * jnp.exp(b_last - b_), v_))
        return new_state, o

    s0 = jnp.zeros((Bs, H, dk, dv), jnp.float32)
    _, o = lax.scan(step, s0, (qc, kc, vc, bc))
    return o.transpose(1, 0, 3, 2, 4).reshape(Bs, S, H, dv)


def setup_inputs(seed: int = 0) -> dict:
    key = jax.random.key(seed)
    ks = jax.random.split(key, 20)
    f32 = jnp.float32

    def nrm(k, shape, scale):
        return jax.random.normal(k, shape, f32) * scale

    def gain(k, shape):
        return 1.0 + 0.05 * jax.random.normal(k, shape, f32)

    return {
        "x": jax.random.normal(ks[0], (BATCH, SEQ, D_MODEL), f32),
        "p": jax.random.normal(ks[1], (DEPTH, BATCH, SEQ, PLE_DIM), f32),
        "norm_mix_g": gain(ks[2], (DEPTH, D_MODEL)),
        "w_in": nrm(ks[3], (DEPTH, D_MODEL, D_IN_PROJ), D_MODEL ** -0.5),
        "conv_w": nrm(ks[4], (DEPTH, CONV_WIDTH, D_CONV), CONV_WIDTH ** -0.5),
        "conv_b": nrm(ks[5], (DEPTH, D_CONV), 0.02),
        "conv_ln_g": gain(ks[6], (DEPTH, D_CONV)),
        "conv_ln_b": nrm(ks[7], (DEPTH, D_CONV), 0.02),
        "lb_logits": gain(ks[8], (DEPTH + 1, D_REC)),
        "rec_norm_g": gain(ks[9], (DEPTH, D_REC)),
        "w_out": nrm(ks[10], (DEPTH, D_MIX, D_MODEL), D_MIX ** -0.5),
        "norm_ffn_g": gain(ks[11], (DEPTH, D_MODEL)),
        "w_up": nrm(ks[12], (DEPTH, D_MODEL, D_FF), D_MODEL ** -0.5),
        "w_down": nrm(ks[13], (DEPTH, D_FF, D_MODEL), D_FF ** -0.5),
        "w_ple": nrm(ks[14], (DEPTH, PLE_DIM, D_MODEL), PLE_DIM ** -0.5),
        "ple_norm_g": gain(ks[15], (DEPTH, D_MODEL)),
        "w_ple_gate": nrm(ks[16], (DEPTH, D_MODEL, D_MODEL), D_MODEL ** -0.5),
        "final_norm_g": gain(ks[17], (D_MODEL,)),
    }


def reference(x, p, norm_mix_g, w_in, conv_w, conv_b, conv_ln_g, conv_ln_b, lb_logits,
              rec_norm_g, w_out, norm_ffn_g, w_up, w_down, w_ple, ple_norm_g, w_ple_gate,
              final_norm_g):
    Bs, S, _ = x.shape
    lower_bounds = jnp.cumsum(jax.nn.softmax(lb_logits.astype(jnp.float32), axis=0), axis=0)
    h = x
    for l in range(DEPTH):
        u = rmsnorm(h, norm_mix_g[l])
        z = u @ w_in[l]
        conv_a, conv_b_in, q, f_pre, i_in, g = jnp.split(
            z, np.cumsum([D_CONV, D_CONV, D_REC, D_REC, D_REC]), axis=-1)

        c = conv_a * jax.nn.sigmoid(conv_b_in)
        c = causal_depthwise_conv(c, conv_w[l], conv_b[l])
        y_conv = jax.nn.silu(group_layernorm(c, conv_ln_g[l], conv_ln_b[l]))

        lb = lower_bounds[l]
        f = lb + (1.0 - lb) * jax.nn.sigmoid(f_pre.astype(jnp.float32))
        logf = jnp.log(f)
        k = 1.0 - f
        shp = (Bs, S, N_REC_HEADS, REC_HEAD_DIM)
        o = hgrn2_chunkwise(q.astype(jnp.float32).reshape(shp), k.reshape(shp),
                            i_in.astype(jnp.float32).reshape(shp), logf.reshape(shp))
        o = o * lax.rsqrt(jnp.mean(o * o, axis=-1, keepdims=True) + EPS)
        o = o.reshape(Bs, S, D_REC) * rec_norm_g[l].astype(jnp.float32)
        y_rec = (o.astype(x.dtype) * jax.nn.silu(g))

        h = h + jnp.concatenate([y_conv, y_rec], axis=-1) @ w_out[l]

        v = rmsnorm(h, norm_ffn_g[l])
        h = h + jnp.square(jax.nn.relu(v @ w_up[l])) @ w_down[l]

        h = h + rmsnorm(p[l] @ w_ple[l], ple_norm_g[l]) * jax.nn.sigmoid(h @ w_ple_gate[l])
    return rmsnorm(h, final_norm_g)
```

```python
import functools

import numpy as np
import jax
import jax.numpy as jnp
from jax import lax
from jax.experimental import pallas as pl
from jax.experimental.pallas import tpu as pltpu

EPS = 1e-6
CONV_GROUPS = 8
REC_HEAD_DIM = 128
CHUNK = 64
CHUNK_LEVELS = 6

SUBLANES = 8
V7X_VMEM_LIMIT_BYTES = 56 * 1024 * 1024

MIX_ROWS = 512
CONV_ROW_BLOCK = 32
FFN_ROWS = 512
FFN_SLAB = 1024

BF16 = jnp.bfloat16
F32 = jnp.float32


def _dot(a, b):
    return jnp.dot(a, b, preferred_element_type=F32)


def _dot_nt(a, b):
    return lax.dot_general(a, b, (((1,), (1,)), ((), ())), preferred_element_type=F32)


def _dot_tn(a, b):
    return lax.dot_general(a, b, (((0,), (0,)), ((), ())), preferred_element_type=F32)


def _rms_scale(x):
    return lax.rsqrt(jnp.mean(x * x, axis=-1, keepdims=True) + EPS)


def _decay_sum_matrix():
    n = CHUNK
    p = np.arange(n)[:, None]
    s = np.arange(n)[None, :]
    blocks = [s <= p, s > p]
    for level in range(CHUNK_LEVELS):
        h = 1 << level
        mid = (p // (2 * h)) * (2 * h) + h
        second = (p % (2 * h)) >= h
        blocks.append(np.where(second, (s >= mid) & (s <= p), (s > p) & (s < mid)))
    d = np.concatenate(blocks, axis=0).astype(np.float32)
    return np.concatenate([d, d], axis=1)


def _pair_masks():
    n = CHUNK
    t = np.arange(n)[:, None]
    s = np.arange(n)[None, :]
    masks = [t == s]
    for level in range(CHUNK_LEVELS):
        h = 1 << level
        masks.append((t // (2 * h) == s // (2 * h)) & ((t % (2 * h)) >= h) & ((s % (2 * h)) < h))
    return np.stack(masks).astype(np.float32)


def _group_mean_matrix(width, group):
    g = np.arange(width) // group
    return (g[:, None] == g[None, :]).astype(np.float32) / group


def _mixer_kernel(layer, conv_width, halo,
                  x_ref, ng_ref, win_ref, cw_ref, cb_ref, lng_ref, lnb_ref, lbl_ref, rg_ref, wout_ref,
                  gconv_ref, grec_ref, dmat_ref, masks_ref,
                  o_ref,
                  cs_ref, yc_ref, q_ref, k_ref, v_ref, g_ref, lf_ref, orec_ref, st_ref):
    rows = x_ref.shape[0]
    d_conv = cs_ref.shape[2]
    d_rec = q_ref.shape[1]
    n_heads = d_rec // REC_HEAD_DIM

    @pl.when(pl.program_id(1) == 0)
    def _():
        cs_ref[0, 0:halo, :] = jnp.zeros((halo, d_conv), F32)
        st_ref[...] = jnp.zeros(st_ref.shape, F32)

    x = x_ref[...]
    u = (x * _rms_scale(x) * ng_ref[...]).astype(BF16)

    zab = _dot(u, win_ref[:, 0:2 * d_conv])
    cs_ref[0, halo:halo + rows, :] = zab[:, :d_conv] * jax.nn.sigmoid(zab[:, d_conv:])
    shifted_len = rows + halo - SUBLANES
    for r in range(1, SUBLANES):
        cs_ref[r, 0:shifted_len, :] = cs_ref[0, r:r + shifted_len, :]

    def conv_block(i, carry):
        r0 = pl.multiple_of(i * CONV_ROW_BLOCK, CONV_ROW_BLOCK)
        acc = jnp.broadcast_to(cb_ref[...], (CONV_ROW_BLOCK, d_conv))
        for tap in range(conv_width):
            a, r = divmod(halo - (conv_width - 1) + tap, SUBLANES)
            acc = acc + cw_ref[tap:tap + 1, :] * cs_ref[r, pl.ds(r0 + SUBLANES * a, CONV_ROW_BLOCK), :]
        yc_ref[pl.ds(r0, CONV_ROW_BLOCK), :] = acc
        return carry

    lax.fori_loop(0, rows // CONV_ROW_BLOCK, conv_block, 0)
    cs_ref[0, 0:halo, :] = cs_ref[0, rows:rows + halo, :]

    yc = yc_ref[...]
    dev = yc - _dot(yc.astype(BF16), gconv_ref[...])
    var = _dot((dev * dev).astype(BF16), gconv_ref[...])
    y_conv = jax.nn.silu(dev * lax.rsqrt(var + EPS) * lng_ref[...] + lnb_ref[...]).astype(BF16)

    c0 = 2 * d_conv
    q_ref[...] = _dot(u, win_ref[:, c0:c0 + d_rec])
    lbl = lbl_ref[...]
    lb_e = jnp.exp(lbl - jnp.max(lbl, axis=0, keepdims=True))
    lb = jnp.sum(lb_e[0:layer + 1], axis=0, keepdims=True) / jnp.sum(lb_e, axis=0, keepdims=True)
    f = lb + (1.0 - lb) * jax.nn.sigmoid(_dot(u, win_ref[:, c0 + d_rec:c0 + 2 * d_rec]))
    k_ref[...] = 1.0 - f
    logf = jnp.log(f)
    logf_hi = logf.astype(BF16)
    logf_lo = (logf - logf_hi.astype(F32)).astype(BF16)
    n_chunks = rows // CHUNK
    lf_ref[:, 0:CHUNK, :] = logf_hi.reshape(n_chunks, CHUNK, d_rec)
    lf_ref[:, CHUNK:2 * CHUNK, :] = logf_lo.reshape(n_chunks, CHUNK, d_rec)
    v_ref[...] = _dot(u, win_ref[:, c0 + 2 * d_rec:c0 + 3 * d_rec])
    g_ref[...] = _dot(u, win_ref[:, c0 + 3 * d_rec:c0 + 4 * d_rec])

    def chunk_body(c, carry):
        r0 = pl.multiple_of(c * CHUNK, CHUNK)
        decay = jnp.exp(_dot(dmat_ref[...], lf_ref[c]))
        q = q_ref[pl.ds(r0, CHUNK), :]
        k = k_ref[pl.ds(r0, CHUNK), :]
        vb = v_ref[pl.ds(r0, CHUNK), :].astype(BF16)
        q_state = (q * decay[0:CHUNK]).astype(BF16)
        k_state = (k * decay[CHUNK:2 * CHUNK]).astype(BF16)
        decay_last = decay[CHUNK - 1:CHUNK, :]
        qb = q.astype(BF16)
        kb = k.astype(BF16)
        row = lax.broadcasted_iota(jnp.int32, (CHUNK, d_rec), 0)
        mixed = []
        for level in range(CHUNK_LEVELS):
            pick_q = (row & (1 << level)) != 0
            lo = (2 + level) * CHUNK
            mixed.append((jnp.where(pick_q, q, k) * decay[lo:lo + CHUNK]).astype(BF16))
        outs = []
        for h in range(n_heads):
            sl = slice(h * REC_HEAD_DIM, (h + 1) * REC_HEAD_DIM)
            attn = _dot_nt(qb[:, sl], kb[:, sl]) * masks_ref[0]
            for level in range(CHUNK_LEVELS):
                m = mixed[level][:, sl]
                attn = attn + _dot_nt(m, m) * masks_ref[level + 1]
            state_t = st_ref[h]
            outs.append(_dot(attn.astype(BF16), vb[:, sl]) + _dot_nt(q_state[:, sl], state_t.astype(BF16)))
            st_ref[h] = state_t * decay_last[:, sl] + _dot_tn(vb[:, sl], k_state[:, sl])
        orec_ref[pl.ds(r0, CHUNK), :] = jnp.concatenate(outs, axis=1)
        return carry

    lax.fori_loop(0, n_chunks, chunk_body, 0)

    o = orec_ref[...]
    o_ms = _dot((o * o).astype(BF16), grec_ref[...])
    y_rec = (o * lax.rsqrt(o_ms + EPS) * rg_ref[...] * jax.nn.silu(g_ref[...])).astype(BF16)

    o_ref[...] = x + _dot(y_conv, wout_ref[0:d_conv, :]) + _dot(y_rec, wout_ref[d_conv:d_conv + d_rec, :])


def _channel_kernel(final,
                    h_ref, p_ref, ng_ref, wup_ref, wdown_ref, wple_ref, pg_ref, wgate_ref, fg_ref,
                    o_ref):
    h = h_ref[...]
    v = (h * _rms_scale(h) * ng_ref[...]).astype(BF16)
    d_ff = wup_ref.shape[1]
    acc = h
    for s0 in range(0, d_ff, FFN_SLAB):
        up = jnp.maximum(_dot(v, wup_ref[:, s0:s0 + FFN_SLAB]), 0.0)
        acc = acc + _dot((up * up).astype(BF16), wdown_ref[s0:s0 + FFN_SLAB, :])
    e = _dot(p_ref[...].astype(BF16), wple_ref[...])
    e = e * _rms_scale(e) * pg_ref[...]
    out = acc + e * jax.nn.sigmoid(_dot(acc.astype(BF16), wgate_ref[...]))
    if final:
        out = out * _rms_scale(out) * fg_ref[...]
    o_ref[...] = out


def _resident(shape):
    return pl.BlockSpec(shape, lambda *_: (0,) * len(shape), pipeline_mode=pl.Buffered(1))


def _mixer(layer, x, norm_g, w_in, conv_w, conv_b, ln_g, ln_b, lb_logits, rec_g, w_out):
    batch, seq, d_model = x.shape
    conv_width, d_conv = conv_w.shape
    d_rec = rec_g.shape[0]
    n_heads = d_rec // REC_HEAD_DIM
    rows = min(MIX_ROWS, seq)
    halo = -(-(conv_width - 1) // SUBLANES) * SUBLANES
    assert seq % rows == 0 and rows % CHUNK == 0 and rows % CONV_ROW_BLOCK == 0 and rows >= halo
    assert d_rec % REC_HEAD_DIM == 0 and w_in.shape == (d_model, 2 * d_conv + 4 * d_rec)
    assert w_out.shape == (d_conv + d_rec, d_model) and d_conv % CONV_GROUPS == 0

    row = lambda a: a.reshape(1, -1)
    consts = (
        jnp.asarray(_group_mean_matrix(d_conv, d_conv // CONV_GROUPS), BF16),
        jnp.asarray(_group_mean_matrix(d_rec, REC_HEAD_DIM), BF16),
        jnp.asarray(_decay_sum_matrix(), BF16),
        jnp.asarray(_pair_masks(), F32),
    )
    params = (row(norm_g), w_in.astype(BF16), conv_w, row(conv_b), row(ln_g), row(ln_b), lb_logits,
              row(rec_g), w_out.astype(BF16))
    tile = pl.BlockSpec((None, rows, d_model), lambda b, j: (b, j, 0))
    return pl.pallas_call(
        functools.partial(_mixer_kernel, layer, conv_width, halo),
        out_shape=jax.ShapeDtypeStruct(x.shape, x.dtype),
        grid=(batch, seq // rows),
        in_specs=[tile] + [_resident(a.shape) for a in params + consts],
        out_specs=tile,
        scratch_shapes=[
            pltpu.VMEM((SUBLANES, rows + halo, d_conv), F32),
            pltpu.VMEM((rows, d_conv), F32),
            pltpu.VMEM((rows, d_rec), F32),
            pltpu.VMEM((rows, d_rec), F32),
            pltpu.VMEM((rows, d_rec), F32),
            pltpu.VMEM((rows, d_rec), F32),
            pltpu.VMEM((rows // CHUNK, 2 * CHUNK, d_rec), BF16),
            pltpu.VMEM((rows, d_rec), F32),
            pltpu.VMEM((n_heads, REC_HEAD_DIM, REC_HEAD_DIM), F32),
        ],
        compiler_params=pltpu.CompilerParams(
            dimension_semantics=("arbitrary", "arbitrary"), vmem_limit_bytes=V7X_VMEM_LIMIT_BYTES),
        name="mixer",
    )(x, *params, *consts)


def _channel(final, h, p, norm_g, w_up, w_down, w_ple, ple_g, w_gate, final_g):
    n, d_model = h.shape
    rows = min(FFN_ROWS, n)
    assert n % rows == 0 and w_up.shape[1] % FFN_SLAB == 0
    row = lambda a: a.reshape(1, -1)
    params = (row(norm_g), w_up.astype(BF16), w_down.astype(BF16), w_ple.astype(BF16), row(ple_g),
              w_gate.astype(BF16), row(final_g))
    return pl.pallas_call(
        functools.partial(_channel_kernel, final),
        out_shape=jax.ShapeDtypeStruct(h.shape, h.dtype),
        grid=(n // rows,),
        in_specs=[pl.BlockSpec((rows, d_model), lambda i: (i, 0)),
                  pl.BlockSpec((rows, p.shape[1]), lambda i: (i, 0))]
        + [_resident(a.shape) for a in params],
        out_specs=pl.BlockSpec((rows, d_model), lambda i: (i, 0)),
        compiler_params=pltpu.CompilerParams(
            dimension_semantics=("arbitrary",), vmem_limit_bytes=V7X_VMEM_LIMIT_BYTES),
        name="channel",
    )(h, p, *params)


def kernel(x, p, norm_mix_g, w_in, conv_w, conv_b, conv_ln_g, conv_ln_b, lb_logits, rec_norm_g, w_out, norm_ffn_g, w_up, w_down, w_ple, ple_norm_g, w_ple_gate, final_norm_g):
    batch, seq, d_model = x.shape
    depth = w_in.shape[0]
    h = x
    for l in range(depth):
        h = _mixer(l, h, norm_mix_g[l], w_in[l], conv_w[l], conv_b[l], conv_ln_g[l], conv_ln_b[l],
                   lb_logits, rec_norm_g[l], w_out[l])
        h = _channel(l == depth - 1, h.reshape(batch * seq, d_model), p[l].reshape(batch * seq, -1),
                     norm_ffn_g[l], w_up[l], w_down[l], w_ple[l], ple_norm_g[l], w_ple_gate[l],
                     final_norm_g).reshape(batch, seq, d_model)
    return h
```

```python
import functools

import numpy as np
import jax
import jax.numpy as jnp
from jax import lax
from jax.experimental import pallas as pl
from jax.experimental.pallas import tpu as pltpu

EPS = 1e-6
CONV_GROUPS = 8
REC_HEAD_DIM = 128
CHUNK = 64
CHUNK_LEVELS = 6
CHUNK_OPERANDS = 4 + CHUNK_LEVELS

SUBLANES = 8
V7X_VMEM_LIMIT_BYTES = 56 * 1024 * 1024

MIX_ROWS = 512
CONV_ROW_BLOCK = 32
FFN_ROWS = 512
FFN_SLAB = 1024

BF16 = jnp.bfloat16
F32 = jnp.float32


def _dot(a, b):
    return jnp.dot(a, b, preferred_element_type=F32)


def _dot_nt(a, b):
    return lax.dot_general(a, b, (((1,), (1,)), ((), ())), preferred_element_type=F32)


def _dot_tn(a, b):
    return lax.dot_general(a, b, (((0,), (0,)), ((), ())), preferred_element_type=F32)


def _rms_scale(x):
    return lax.rsqrt(jnp.mean(x * x, axis=-1, keepdims=True) + EPS)


def _decay_sum_matrix():
    n = CHUNK
    p = np.arange(n)[:, None]
    s = np.arange(n)[None, :]
    blocks = [s <= p, s > p]
    for level in range(CHUNK_LEVELS):
        h = 1 << level
        mid = (p // (2 * h)) * (2 * h) + h
        second = (p % (2 * h)) >= h
        blocks.append(np.where(second, (s >= mid) & (s <= p), (s > p) & (s < mid)))
    d = np.concatenate(blocks, axis=0).astype(np.float32)
    return np.concatenate([d, d], axis=1)


def _pair_masks():
    n = CHUNK
    t = np.arange(n)[:, None]
    s = np.arange(n)[None, :]
    masks = [t == s]
    for level in range(CHUNK_LEVELS):
        h = 1 << level
        masks.append((t // (2 * h) == s // (2 * h)) & ((t % (2 * h)) >= h) & ((s % (2 * h)) < h))
    return np.stack(masks).astype(np.float32)


def _group_mean_matrix(width, group):
    g = np.arange(width) // group
    return (g[:, None] == g[None, :]).astype(np.float32) / group


def _mixer_kernel(layer, conv_width, halo,
                  x_ref, ng_ref, win_ref, cw_ref, cb_ref, lng_ref, lnb_ref, lbl_ref, rg_ref, wout_ref,
                  gconv_ref, grec_ref, dmat_ref, masks_ref,
                  o_ref,
                  cs_ref, yc_ref, q_ref, k_ref, vb_ref, g_ref, lf_ref, orec_ref, st_ref,
                  opa_ref, dla_ref, opb_ref, dlb_ref):
    rows = x_ref.shape[0]
    d_conv = cs_ref.shape[2]
    d_rec = q_ref.shape[1]
    n_heads = d_rec // REC_HEAD_DIM

    @pl.when(pl.program_id(1) == 0)
    def _():
        cs_ref[0, 0:halo, :] = jnp.zeros((halo, d_conv), F32)
        st_ref[...] = jnp.zeros(st_ref.shape, F32)

    x = x_ref[...]
    u = (x * _rms_scale(x) * ng_ref[...]).astype(BF16)

    zab = _dot(u, win_ref[:, 0:2 * d_conv])
    cs_ref[0, halo:halo + rows, :] = zab[:, :d_conv] * jax.nn.sigmoid(zab[:, d_conv:])
    shifted_len = rows + halo - SUBLANES
    for r in range(1, SUBLANES):
        cs_ref[r, 0:shifted_len, :] = cs_ref[0, r:r + shifted_len, :]

    def conv_block(i, carry):
        r0 = pl.multiple_of(i * CONV_ROW_BLOCK, CONV_ROW_BLOCK)
        acc = jnp.broadcast_to(cb_ref[...], (CONV_ROW_BLOCK, d_conv))
        for tap in range(conv_width):
            a, r = divmod(halo - (conv_width - 1) + tap, SUBLANES)
            acc = acc + cw_ref[tap:tap + 1, :] * cs_ref[r, pl.ds(r0 + SUBLANES * a, CONV_ROW_BLOCK), :]
        yc_ref[pl.ds(r0, CONV_ROW_BLOCK), :] = acc
        return carry

    lax.fori_loop(0, rows // CONV_ROW_BLOCK, conv_block, 0)
    cs_ref[0, 0:halo, :] = cs_ref[0, rows:rows + halo, :]

    yc = yc_ref[...]
    dev = yc - _dot(yc.astype(BF16), gconv_ref[...])
    var = _dot((dev * dev).astype(BF16), gconv_ref[...])
    y_conv = jax.nn.silu(dev * lax.rsqrt(var + EPS) * lng_ref[...] + lnb_ref[...]).astype(BF16)

    c0 = 2 * d_conv
    q_ref[...] = _dot(u, win_ref[:, c0:c0 + d_rec])
    lbl = lbl_ref[...]
    lb_e = jnp.exp(lbl - jnp.max(lbl, axis=0, keepdims=True))
    lb = jnp.sum(lb_e[0:layer + 1], axis=0, keepdims=True) / jnp.sum(lb_e, axis=0, keepdims=True)
    f = lb + (1.0 - lb) * jax.nn.sigmoid(_dot(u, win_ref[:, c0 + d_rec:c0 + 2 * d_rec]))
    k_ref[...] = 1.0 - f
    logf = jnp.log(f)
    logf_hi = logf.astype(BF16)
    logf_lo = (logf - logf_hi.astype(F32)).astype(BF16)
    n_chunks = rows // CHUNK
    lf_ref[:, 0:CHUNK, :] = logf_hi.reshape(n_chunks, CHUNK, d_rec)
    lf_ref[:, CHUNK:2 * CHUNK, :] = logf_lo.reshape(n_chunks, CHUNK, d_rec)
    vb_ref[...] = _dot(u, win_ref[:, c0 + 2 * d_rec:c0 + 3 * d_rec]).astype(BF16)
    g_ref[...] = _dot(u, win_ref[:, c0 + 3 * d_rec:c0 + 4 * d_rec])

    def chunk_operands(c, op_ref, dl_ref):
        r0 = pl.multiple_of(c * CHUNK, CHUNK)
        decay = jnp.exp(_dot(dmat_ref[...], lf_ref[c]))
        q = q_ref[pl.ds(r0, CHUNK), :]
        k = k_ref[pl.ds(r0, CHUNK), :]
        op_ref[0] = q.astype(BF16)
        op_ref[1] = k.astype(BF16)
        op_ref[2] = (q * decay[0:CHUNK]).astype(BF16)
        op_ref[3] = (k * decay[CHUNK:2 * CHUNK]).astype(BF16)
        dl_ref[...] = decay[CHUNK - 1:CHUNK, :]
        row = lax.broadcasted_iota(jnp.int32, (CHUNK, d_rec), 0)
        for level in range(CHUNK_LEVELS):
            pick_q = (row & (1 << level)) != 0
            lo = (2 + level) * CHUNK
            op_ref[4 + level] = (jnp.where(pick_q, q, k) * decay[lo:lo + CHUNK]).astype(BF16)

    def chunk_attend(c, op_ref, dl_ref):
        r0 = pl.multiple_of(c * CHUNK, CHUNK)
        heads = [slice(h * REC_HEAD_DIM, (h + 1) * REC_HEAD_DIM) for h in range(n_heads)]
        attn = [_dot_nt(op_ref[0, :, sl], op_ref[1, :, sl]) * masks_ref[0] for sl in heads]
        for level in range(CHUNK_LEVELS):
            for h, sl in enumerate(heads):
                m = op_ref[4 + level, :, sl]
                attn[h] = attn[h] + _dot_nt(m, m) * masks_ref[level + 1]
        outs = []
        for h, sl in enumerate(heads):
            vb = vb_ref[pl.ds(r0, CHUNK), sl]
            state_t = st_ref[h]
            outs.append(_dot(attn[h].astype(BF16), vb) + _dot_nt(op_ref[2, :, sl], state_t.astype(BF16)))
            st_ref[h] = state_t * dl_ref[:, sl] + _dot_tn(vb, op_ref[3, :, sl])
        orec_ref[pl.ds(r0, CHUNK), :] = jnp.concatenate(outs, axis=1)

    chunk_operands(0, opa_ref, dla_ref)

    def chunk_pair(i, carry):
        chunk_operands(2 * i + 1, opb_ref, dlb_ref)
        chunk_attend(2 * i, opa_ref, dla_ref)
        chunk_operands(2 * i + 2, opa_ref, dla_ref)
        chunk_attend(2 * i + 1, opb_ref, dlb_ref)
        return carry

    lax.fori_loop(0, n_chunks // 2 - 1, chunk_pair, 0)
    chunk_operands(n_chunks - 1, opb_ref, dlb_ref)
    chunk_attend(n_chunks - 2, opa_ref, dla_ref)
    chunk_attend(n_chunks - 1, opb_ref, dlb_ref)

    o = orec_ref[...]
    o_ms = _dot((o * o).astype(BF16), grec_ref[...])
    y_rec = (o * lax.rsqrt(o_ms + EPS) * rg_ref[...] * jax.nn.silu(g_ref[...])).astype(BF16)

    o_ref[...] = x + _dot(y_conv, wout_ref[0:d_conv, :]) + _dot(y_rec, wout_ref[d_conv:d_conv + d_rec, :])


def _channel_kernel(final,
                    h_ref, p_ref, ng_ref, wup_ref, wdown_ref, wple_ref, pg_ref, wgate_ref, fg_ref,
                    o_ref):
    h = h_ref[...]
    v = (h * _rms_scale(h) * ng_ref[...]).astype(BF16)
    d_ff = wup_ref.shape[1]
    acc = h
    for s0 in range(0, d_ff, FFN_SLAB):
        up = jnp.maximum(_dot(v, wup_ref[:, s0:s0 + FFN_SLAB]), 0.0)
        acc = acc + _dot((up * up).astype(BF16), wdown_ref[s0:s0 + FFN_SLAB, :])
    e = _dot(p_ref[...].astype(BF16), wple_ref[...])
    e = e * _rms_scale(e) * pg_ref[...]
    out = acc + e * jax.nn.sigmoid(_dot(acc.astype(BF16), wgate_ref[...]))
    if final:
        out = out * _rms_scale(out) * fg_ref[...]
    o_ref[...] = out


def _resident(shape):
    return pl.BlockSpec(shape, lambda *_: (0,) * len(shape), pipeline_mode=pl.Buffered(1))


def _mixer(layer, x, norm_g, w_in, conv_w, conv_b, ln_g, ln_b, lb_logits, rec_g, w_out):
    batch, seq, d_model = x.shape
    conv_width, d_conv = conv_w.shape
    d_rec = rec_g.shape[0]
    n_heads = d_rec // REC_HEAD_DIM
    rows = min(MIX_ROWS, seq)
    halo = -(-(conv_width - 1) // SUBLANES) * SUBLANES
    assert seq % rows == 0 and rows % (2 * CHUNK) == 0 and rows % CONV_ROW_BLOCK == 0 and rows >= halo
    assert d_rec % REC_HEAD_DIM == 0 and w_in.shape == (d_model, 2 * d_conv + 4 * d_rec)
    assert w_out.shape == (d_conv + d_rec, d_model) and d_conv % CONV_GROUPS == 0

    row = lambda a: a.reshape(1, -1)
    consts = (
        jnp.asarray(_group_mean_matrix(d_conv, d_conv // CONV_GROUPS), BF16),
        jnp.asarray(_group_mean_matrix(d_rec, REC_HEAD_DIM), BF16),
        jnp.asarray(_decay_sum_matrix(), BF16),
        jnp.asarray(_pair_masks(), F32),
    )
    params = (row(norm_g), w_in.astype(BF16), conv_w, row(conv_b), row(ln_g), row(ln_b), lb_logits,
              row(rec_g), w_out.astype(BF16))
    tile = pl.BlockSpec((None, rows, d_model), lambda b, j: (b, j, 0))
    return pl.pallas_call(
        functools.partial(_mixer_kernel, layer, conv_width, halo),
        out_shape=jax.ShapeDtypeStruct(x.shape, x.dtype),
        grid=(batch, seq // rows),
        in_specs=[tile] + [_resident(a.shape) for a in params + consts],
        out_specs=tile,
        scratch_shapes=[
            pltpu.VMEM((SUBLANES, rows + halo, d_conv), F32),
            pltpu.VMEM((rows, d_conv), F32),
            pltpu.VMEM((rows, d_rec), F32),
            pltpu.VMEM((rows, d_rec), F32),
            pltpu.VMEM((rows, d_rec), BF16),
            pltpu.VMEM((rows, d_rec), F32),
            pltpu.VMEM((rows // CHUNK, 2 * CHUNK, d_rec), BF16),
            pltpu.VMEM((rows, d_rec), F32),
            pltpu.VMEM((n_heads, REC_HEAD_DIM, REC_HEAD_DIM), F32),
            pltpu.VMEM((CHUNK_OPERANDS, CHUNK, d_rec), BF16), pltpu.VMEM((1, d_rec), F32),
            pltpu.VMEM((CHUNK_OPERANDS, CHUNK, d_rec), BF16), pltpu.VMEM((1, d_rec), F32),
        ],
        compiler_params=pltpu.CompilerParams(
            dimension_semantics=("arbitrary", "arbitrary"), vmem_limit_bytes=V7X_VMEM_LIMIT_BYTES),
        name="mixer",
    )(x, *params, *consts)


def _channel(final, h, p, norm_g, w_up, w_down, w_ple, ple_g, w_gate, final_g):
    n, d_model = h.shape
    rows = min(FFN_ROWS, n)
    assert n % rows == 0 and w_up.shape[1] % FFN_SLAB == 0
    row = lambda a: a.reshape(1, -1)
    params = (row(norm_g), w_up.astype(BF16), w_down.astype(BF16), w_ple.astype(BF16), row(ple_g),
              w_gate.astype(BF16), row(final_g))
    return pl.pallas_call(
        functools.partial(_channel_kernel, final),
        out_shape=jax.ShapeDtypeStruct(h.shape, h.dtype),
        grid=(n // rows,),
        in_specs=[pl.BlockSpec((rows, d_model), lambda i: (i, 0)),
                  pl.BlockSpec((rows, p.shape[1]), lambda i: (i, 0))]
        + [_resident(a.shape) for a in params],
        out_specs=pl.BlockSpec((rows, d_model), lambda i: (i, 0)),
        compiler_params=pltpu.CompilerParams(
            dimension_semantics=("arbitrary",), vmem_limit_bytes=V7X_VMEM_LIMIT_BYTES),
        name="channel",
    )(h, p, *params)


def kernel(x, p, norm_mix_g, w_in, conv_w, conv_b, conv_ln_g, conv_ln_b, lb_logits, rec_norm_g, w_out, norm_ffn_g, w_up, w_down, w_ple, ple_norm_g, w_ple_gate, final_norm_g):
    batch, seq, d_model = x.shape
    depth = w_in.shape[0]
    h = x
    for l in range(depth):
        h = _mixer(l, h, norm_mix_g[l], w_in[l], conv_w[l], conv_b[l], conv_ln_g[l], conv_ln_b[l],
                   lb_logits, rec_norm_g[l], w_out[l])
        h = _channel(l == depth - 1, h.reshape(batch * seq, d_model), p[l].reshape(batch * seq, -1),
                     norm_ffn_g[l], w_up[l], w_down[l], w_ple[l], ple_norm_g[l], w_ple_gate[l],
                     final_norm_g).reshape(batch, seq, d_model)
    return h
```
